```python
import jax, jax.numpy as jnp
from jax import lax
import numpy as np

D_MODEL = 1024
BATCH = 4
SEQ = 4096
DEPTH = 2

GRID_W = 64
CTX_LEN = 256
POOL_WIDTH = 512
POOL_GROUPS = 4
POOL_GROUP_DIM = POOL_WIDTH // POOL_GROUPS
POOL_WINDOWS = (2, 4, 8, 16)
N_HEADS = 8
N_KV_HEADS = 2
HEAD_DIM = 64
Q_GROUP = N_HEADS // N_KV_HEADS
ATTN_WIDTH = N_HEADS * HEAD_DIM
KV_WIDTH = N_KV_HEADS * HEAD_DIM
MIX_WIDTH = POOL_WIDTH + ATTN_WIDTH
PROJ_WIDTH = POOL_WIDTH + ATTN_WIDTH + 2 * KV_WIDTH
WINDOW = 128
BLOCK = 128
ROPE_BASE = 10000.0
ROPE_AXIS_DIM = HEAD_DIM // 2
D_FF = 2816
N_MOD = 9
EPS = 1e-6
NEG_INF = -1e30

kernel_name = "hybrid_pool_swa_macaron_dit_block"


def rmsnorm(x, g):
    xf = x.astype(jnp.float32)
    y = xf * lax.rsqrt(jnp.mean(xf * xf, axis=-1, keepdims=True) + EPS)
    return (y * g.astype(jnp.float32)).astype(x.dtype)


def norm_modulate(x, g, shift, scale):
    return rmsnorm(x, g) * (1 + scale) + shift


def swiglu(n, w_in, w_out):
    a, b = jnp.split(n @ w_in, 2, axis=-1)
    return (jax.nn.silu(a) * b) @ w_out


def axial_rope_tables(T):
    rows = T // GRID_W
    row = jnp.repeat(jnp.arange(rows), GRID_W).astype(jnp.float32)
    col = jnp.tile(jnp.arange(GRID_W), rows).astype(jnp.float32)
    inv = ROPE_BASE ** (-jnp.arange(0, ROPE_AXIS_DIM, 2, dtype=jnp.float32) / ROPE_AXIS_DIM)
    ang = jnp.concatenate([row[:, None] * inv, col[:, None] * inv], axis=-1)
    return jnp.cos(ang), jnp.sin(ang)


def apply_rope(x, cos, sin):
    xf = x.astype(jnp.float32)
    x1, x2 = xf[..., :HEAD_DIM // 2], xf[..., HEAD_DIM // 2:]
    c, s = cos[None, :, None, :], sin[None, :, None, :]
    return jnp.concatenate([x1 * c - x2 * s, x2 * c + x1 * s], axis=-1).astype(x.dtype)


def pool_mixer(u, w_pool, pool_scale):
    B, T, _ = u.shape
    uf = u.astype(jnp.float32)
    cs = jnp.pad(jnp.cumsum(uf, axis=1), ((0, 0), (1, 0), (0, 0)))
    t = jnp.arange(T)
    outs = []
    for g, w in enumerate(POOL_WINDOWS):
        lo = jnp.clip(t - w // 2, 0, T)
        hi = jnp.clip(t + w - w // 2, 0, T)
        csg = cs[..., g * POOL_GROUP_DIM:(g + 1) * POOL_GROUP_DIM]
        outs.append((csg[:, hi] - csg[:, lo]) / (hi - lo).astype(jnp.float32)[None, :, None])
    pooled = (jnp.concatenate(outs, axis=-1) - uf).astype(u.dtype)
    pooled = pooled.reshape(B, T, POOL_GROUPS, POOL_GROUP_DIM)
    mixed = jnp.einsum("btgc,gcd->btgd", pooled, w_pool).reshape(B, T, POOL_WIDTH)
    return mixed * pool_scale


def latent_attention(q, k, v, kc, vc, sink):
    B, T = q.shape[:2]
    nb = T // BLOCK
    scale = HEAD_DIM ** -0.5
    qb = q.reshape(B, nb, BLOCK, N_KV_HEADS, Q_GROUP, HEAD_DIM)
    pad = ((0, 0), (BLOCK, BLOCK), (0, 0), (0, 0))
    kp = jnp.pad(k, pad).reshape(B, nb + 2, BLOCK, N_KV_HEADS, HEAD_DIM)
    vp = jnp.pad(v, pad).reshape(B, nb + 2, BLOCK, N_KV_HEADS, HEAD_DIM)
    kb = jnp.concatenate([kp[:, :-2], kp[:, 1:-1], kp[:, 2:]], axis=2)
    vb = jnp.concatenate([vp[:, :-2], vp[:, 1:-1], vp[:, 2:]], axis=2)
    s_loc = jnp.einsum("bnqhgd,bnkhd->bnhgqk", qb, kb).astype(jnp.float32) * scale
    n_idx = jnp.arange(nb)[:, None, None]
    qpos = n_idx * BLOCK + jnp.arange(BLOCK)[None, :, None]
    kpos = n_idx * BLOCK + jnp.arange(3 * BLOCK)[None, None, :] - BLOCK
    valid = (kpos >= 0) & (kpos < T) & (jnp.abs(kpos - qpos) <= WINDOW)
    s_loc = jnp.where(valid[None, :, None, None], s_loc, NEG_INF)
    s_ctx = jnp.einsum("bnqhgd,bmhd->bnhgqm", qb, kc).astype(jnp.float32) * scale
    s_sink = jnp.broadcast_to(sink.astype(jnp.float32).reshape(1, 1, N_KV_HEADS, Q_GROUP, 1, 1),
                              s_loc.shape[:-1] + (1,))
    p = jax.nn.softmax(jnp.concatenate([s_loc, s_ctx, s_sink], axis=-1), axis=-1)
    L = kc.shape[1]
    p_loc = p[..., :3 * BLOCK].astype(v.dtype)
    p_ctx = p[..., 3 * BLOCK:3 * BLOCK + L].astype(v.dtype)
    o = jnp.einsum("bnhgqk,bnkhd->bnqhgd", p_loc, vb) + jnp.einsum("bnhgqm,bmhd->bnqhgd", p_ctx, vc)
    return o.reshape(B, T, ATTN_WIDTH)


def context_attention(qc, kc, vc, sink):
    B, L = qc.shape[:2]
    qg = qc.reshape(B, L, N_KV_HEADS, Q_GROUP, HEAD_DIM)
    s = jnp.einsum("blhgd,bmhd->bhglm", qg, kc).astype(jnp.float32) * HEAD_DIM ** -0.5
    s_sink = jnp.broadcast_to(sink.astype(jnp.float32).reshape(1, N_KV_HEADS, Q_GROUP, 1, 1),
                              s.shape[:-1] + (1,))
    p = jax.nn.softmax(jnp.concatenate([s, s_sink], axis=-1), axis=-1)[..., :L].astype(vc.dtype)
    return jnp.einsum("bhglm,bmhd->blhgd", p, vc).reshape(B, L, ATTN_WIDTH)


def context_kv(nc, w_in):
    B, L = nc.shape[:2]
    kc, vc = jnp.split(nc @ w_in[:, MIX_WIDTH:], 2, axis=-1)
    return (kc.reshape(B, L, N_KV_HEADS, HEAD_DIM), vc.reshape(B, L, N_KV_HEADS, HEAD_DIM))


def mix_latent(n, kc, vc, w_in, w_pool, pool_scale, sink, w_out, cos, sin):
    B, T = n.shape[:2]
    u, q, k, v = jnp.split(n @ w_in, [POOL_WIDTH, MIX_WIDTH, MIX_WIDTH + KV_WIDTH], axis=-1)
    pool_out = pool_mixer(u, w_pool, pool_scale)
    q = apply_rope(q.reshape(B, T, N_HEADS, HEAD_DIM), cos, sin)
    k = apply_rope(k.reshape(B, T, N_KV_HEADS, HEAD_DIM), cos, sin)
    v = v.reshape(B, T, N_KV_HEADS, HEAD_DIM)
    attn_out = latent_attention(q, k, v, kc, vc, sink)
    return jnp.concatenate([pool_out, attn_out], axis=-1) @ w_out


def mix_context(nc, kc, vc, w_in, w_pool, pool_scale, sink, w_out):
    B, L = nc.shape[:2]
    u, q = jnp.split(nc @ w_in[:, :MIX_WIDTH], [POOL_WIDTH], axis=-1)
    pool_out = pool_mixer(u, w_pool, pool_scale)
    attn_out = context_attention(q.reshape(B, L, N_HEADS, HEAD_DIM), kc, vc, sink)
    return jnp.concatenate([pool_out, attn_out], axis=-1) @ w_out


def setup_inputs(seed: int = 0) -> dict:
    key = jax.random.key(seed)
    ks = jax.random.split(key, 24)
    f32 = jnp.float32
    nrm = lambda k, shape, s: jax.random.normal(k, shape, f32) * s
    gain = lambda k, shape: 1.0 + 0.1 * jax.random.normal(k, shape, f32)
    return {
        "x": nrm(ks[0], (BATCH, SEQ, D_MODEL), 1.0),
        "c": nrm(ks[1], (BATCH, D_MODEL), 1.0),
        "ctx": nrm(ks[2], (BATCH, CTX_LEN, D_MODEL), 1.0),
        "c_ctx": nrm(ks[3], (D_MODEL,), 1.0),
        "w_mod": nrm(ks[4], (DEPTH, D_MODEL, N_MOD * D_MODEL), D_MODEL ** -0.5),
        "b_mod": nrm(ks[5], (DEPTH, N_MOD * D_MODEL), 0.02),
        "norm_ffn1": gain(ks[6], (DEPTH, D_MODEL)),
        "w_ffn1_in": nrm(ks[7], (DEPTH, D_MODEL, 2 * D_FF), D_MODEL ** -0.5),
        "w_ffn1_out": nrm(ks[8], (DEPTH, D_FF, D_MODEL), D_FF ** -0.5),
        "norm_mix": gain(ks[9], (DEPTH, D_MODEL)),
        "w_in": nrm(ks[10], (DEPTH, D_MODEL, PROJ_WIDTH), D_MODEL ** -0.5),
        "w_pool": nrm(ks[11], (DEPTH, POOL_GROUPS, POOL_GROUP_DIM, POOL_GROUP_DIM), POOL_GROUP_DIM ** -0.5),
        "pool_scale": gain(ks[12], (DEPTH, POOL_WIDTH)),
        "sink": nrm(ks[13], (DEPTH, N_HEADS), 1.0),
        "w_out": nrm(ks[14], (DEPTH, MIX_WIDTH, D_MODEL), MIX_WIDTH ** -0.5),
        "norm_ffn2": gain(ks[15], (DEPTH, D_MODEL)),
        "w_ffn2_in": nrm(ks[16], (DEPTH, D_MODEL, 2 * D_FF), D_MODEL ** -0.5),
        "w_ffn2_out": nrm(ks[17], (DEPTH, D_FF, D_MODEL), D_FF ** -0.5),
        "norm_final": gain(ks[18], (D_MODEL,)),
    }


def reference(x, c, ctx, c_ctx, w_mod, b_mod, norm_ffn1, w_ffn1_in, w_ffn1_out, norm_mix, w_in,
              w_pool, pool_scale, sink, w_out, norm_ffn2, w_ffn2_in, w_ffn2_out, norm_final):
    B = x.shape[0]
    cos, sin = axial_rope_tables(x.shape[1])
    h, hc = x, ctx
    for l in range(DEPTH):
        last = l == DEPTH - 1
        mx = (jax.nn.silu(c) @ w_mod[l] + b_mod[l]).reshape(B, N_MOD, 1, D_MODEL)
        mc = (jax.nn.silu(c_ctx) @ w_mod[l] + b_mod[l]).reshape(N_MOD, D_MODEL)
        h = h + 0.5 * mx[:, 2] * swiglu(norm_modulate(h, norm_ffn1[l], mx[:, 0], mx[:, 1]),
                                         w_ffn1_in[l], w_ffn1_out[l])
        hc = hc + 0.5 * mc[2] * swiglu(norm_modulate(hc, norm_ffn1[l], mc[0], mc[1]),
                                        w_ffn1_in[l], w_ffn1_out[l])
        n = norm_modulate(h, norm_mix[l], mx[:, 3], mx[:, 4])
        nc = norm_modulate(hc, norm_mix[l], mc[3], mc[4])
        kc, vc = context_kv(nc, w_in[l])
        h = h + mx[:, 5] * mix_latent(n, kc, vc, w_in[l], w_pool[l], pool_scale[l], sink[l], w_out[l], cos, sin)
        if not last:
            hc = hc + mc[5] * mix_context(nc, kc, vc, w_in[l], w_pool[l], pool_scale[l], sink[l], w_out[l])
            hc = hc + 0.5 * mc[8] * swiglu(norm_modulate(hc, norm_ffn2[l], mc[6], mc[7]),
                                            w_ffn2_in[l], w_ffn2_out[l])
        h = h + 0.5 * mx[:, 8] * swiglu(norm_modulate(h, norm_ffn2[l], mx[:, 6], mx[:, 7]),
                                         w_ffn2_in[l], w_ffn2_out[l])
    return rmsnorm(h, norm_final)
```

```python
import functools

import jax
import jax.numpy as jnp
from jax import lax
from jax.experimental import pallas as pl
from jax.experimental.pallas import tpu as pltpu

D_MODEL = 1024
GRID_W = 64
POOL_WIDTH = 512
POOL_GROUPS = 4
POOL_GROUP_DIM = POOL_WIDTH // POOL_GROUPS
POOL_WINDOWS = (2, 4, 8, 16)
POOL_HALO = 8
N_HEADS = 8
N_KV_HEADS = 2
HEAD_DIM = 64
Q_GROUP = N_HEADS // N_KV_HEADS
ATTN_WIDTH = N_HEADS * HEAD_DIM
KV_WIDTH = N_KV_HEADS * HEAD_DIM
MIX_WIDTH = POOL_WIDTH + ATTN_WIDTH
PROJ_WIDTH = MIX_WIDTH + 2 * KV_WIDTH
BLOCK = 128
ROPE_BASE = 10000.0
ROPE_AXIS_DIM = HEAD_DIM // 2
D_FF = 2816
N_MOD = 9
EPS = 1e-6
NEG_INF = -1e30

LANES = 128
MOD_ROWS = 8
VMEM_LIMIT_BYTES = 60000 * 1024

FFN_TILE = 512
FFN_CHUNK = 256
PROJ_TILE = 1024
MIX_TILE = 512
MOD_COLS = 1024

BF16 = jnp.bfloat16
F32 = jnp.float32


def _dot(a, b):
    return jnp.dot(a, b, preferred_element_type=F32)


def _dot_t(a, b):
    return lax.dot_general(a, b, (((1,), (1,)), ((), ())), preferred_element_type=F32)


def _norm_modulate(x, gain, shift, scale):
    ms = jnp.mean(x * x, axis=-1, keepdims=True)
    return (x * lax.rsqrt(ms + EPS) * gain) * (1.0 + scale) + shift


def _mod_kernel(c_ref, w_ref, b_ref, o_ref):
    c = c_ref[...]
    a = (c * jax.nn.sigmoid(c)).astype(BF16)
    o_ref[...] = _dot(a, w_ref[...].astype(BF16)) + b_ref[...]


def _mod_call(cond, w_mod, b_mod):
    depth = w_mod.shape[0]
    n_out = w_mod.shape[2]
    return pl.pallas_call(
        _mod_kernel,
        grid=(depth, n_out // MOD_COLS),
        in_specs=[
            pl.BlockSpec((MOD_ROWS, D_MODEL), lambda l, j: (0, 0)),
            pl.BlockSpec((None, D_MODEL, MOD_COLS), lambda l, j: (l, 0, j)),
            pl.BlockSpec((None, 1, MOD_COLS), lambda l, j: (l, 0, j)),
        ],
        out_specs=pl.BlockSpec((None, MOD_ROWS, MOD_COLS), lambda l, j: (l, 0, j)),
        out_shape=jax.ShapeDtypeStruct((depth, MOD_ROWS, n_out), F32),
        compiler_params=pltpu.CompilerParams(
            dimension_semantics=("arbitrary", "arbitrary"),
            vmem_limit_bytes=VMEM_LIMIT_BYTES),
        name="adaln_mod",
    )(cond, w_mod, b_mod.reshape(depth, 1, n_out))


def _ffn_kernel(x_ref, mod_ref, gain_ref, win_ref, wout_ref, *rest, row0, final):
    if final:
        gfin_ref, o_ref, act_ref = rest
    else:
        o_ref, act_ref = rest
    x = x_ref[...]
    n = _norm_modulate(x, gain_ref[...], mod_ref[row0:row0 + 1, :],
                       mod_ref[row0 + 1:row0 + 2, :]).astype(BF16)
    for c0 in range(0, D_FF, FFN_CHUNK):
        a = _dot(n, win_ref[:, c0:c0 + FFN_CHUNK])
        b = _dot(n, win_ref[:, D_FF + c0:D_FF + c0 + FFN_CHUNK])
        act_ref[:, c0:c0 + FFN_CHUNK] = (a * jax.nn.sigmoid(a) * b).astype(BF16)
    y = _dot(act_ref[...], wout_ref[...])
    h = x + (0.5 * mod_ref[row0 + 2:row0 + 3, :]) * y
    if final:
        ms = jnp.mean(h * h, axis=-1, keepdims=True)
        h = h * lax.rsqrt(ms + EPS) * gfin_ref[...]
    o_ref[...] = h


def _ffn_call(x, mod, gain, w_in, w_out, *, row0, mod_base, tiles_per_mod, tile,
              final_gain=None):
    n_tok = x.shape[0]
    final = final_gain is not None
    in_specs = [
        pl.BlockSpec((tile, D_MODEL), lambda i: (i, 0)),
        pl.BlockSpec((None, N_MOD, D_MODEL), lambda i: (mod_base + i // tiles_per_mod, 0, 0)),
        pl.BlockSpec((1, D_MODEL), lambda i: (0, 0)),
        pl.BlockSpec((D_MODEL, 2 * D_FF), lambda i: (0, 0)),
        pl.BlockSpec((D_FF, D_MODEL), lambda i: (0, 0)),
    ]
    args = [x, mod, gain, w_in, w_out]
    if final:
        in_specs.append(pl.BlockSpec((1, D_MODEL), lambda i: (0, 0)))
        args.append(final_gain)
    return pl.pallas_call(
        functools.partial(_ffn_kernel, row0=row0, final=final),
        grid=(n_tok // tile,),
        in_specs=in_specs,
        out_specs=pl.BlockSpec((tile, D_MODEL), lambda i: (i, 0)),
        out_shape=jax.ShapeDtypeStruct((n_tok, D_MODEL), F32),
        scratch_shapes=[pltpu.VMEM((tile, D_FF), BF16)],
        compiler_params=pltpu.CompilerParams(
            dimension_semantics=("arbitrary",),
            vmem_limit_bytes=VMEM_LIMIT_BYTES),
        name="ffn_final" if final else "ffn",
    )(*args)


def _proj_kernel(x_ref, mod_ref, gain_ref, w_ref, *rest, rope):
    if rope:
        cos_ref, sin_ref, u_ref, q_ref, kk_ref, vv_ref = rest
    else:
        u_ref, q_ref, kk_ref, vv_ref = rest
    n = _norm_modulate(x_ref[...], gain_ref[...], mod_ref[3:4, :], mod_ref[4:5, :]).astype(BF16)
    p = _dot(n, w_ref[...])
    u_ref[...] = p[:, :POOL_WIDTH]
    k = p[:, MIX_WIDTH:MIX_WIDTH + KV_WIDTH]
    v = p[:, MIX_WIDTH + KV_WIDTH:]
    qs = [p[:, POOL_WIDTH + j * LANES:POOL_WIDTH + (j + 1) * LANES] for j in range(Q_GROUP)]
    if rope:
        c = cos_ref[...]
        s = sin_ref[...]
        k = k * c + pltpu.roll(k, LANES // 2, 1) * s
        qs = [q * c + pltpu.roll(q, LANES // 2, 1) * s for q in qs]
    scale = HEAD_DIM ** -0.5
    q_ref[...] = jnp.concatenate([q * scale for q in qs], axis=1).astype(BF16)
    lane = lax.broadcasted_iota(jnp.int32, k.shape, 1)
    kv0_lanes = (lane % (LANES // 2)) < ROPE_AXIS_DIM
    kk_ref[...] = jnp.concatenate(
        [jnp.where(kv0_lanes, k, 0.0), jnp.where(kv0_lanes, 0.0, k)], axis=1).astype(BF16)
    low = lane < HEAD_DIM
    vv_ref[...] = jnp.concatenate(
        [jnp.where(low, v, 0.0), jnp.where(low, 0.0, v)], axis=1).astype(BF16)


def _proj_call(x, mod, gain, w_in, cos_t, sin_t, *, mod_base, tiles_per_mod, tile, rope):
    n_tok = x.shape[0]
    in_specs = [
        pl.BlockSpec((tile, D_MODEL), lambda i: (i, 0)),
        pl.BlockSpec((None, N_MOD, D_MODEL), lambda i: (mod_base + i // tiles_per_mod, 0, 0)),
        pl.BlockSpec((1, D_MODEL), lambda i: (0, 0)),
        pl.BlockSpec((D_MODEL, PROJ_WIDTH), lambda i: (0, 0)),
    ]
    args = [x, mod, gain, w_in]
    if rope:
        in_specs += [pl.BlockSpec((tile, LANES), lambda i: (i % tiles_per_mod, 0))] * 2
        args += [cos_t, sin_t]
    return pl.pallas_call(
        functools.partial(_proj_kernel, rope=rope),
        grid=(n_tok // tile,),
        in_specs=in_specs,
        out_specs=[
            pl.BlockSpec((tile, POOL_WIDTH), lambda i: (i, 0)),
            pl.BlockSpec((tile, ATTN_WIDTH), lambda i: (i, 0)),
            pl.BlockSpec((tile, 2 * KV_WIDTH), lambda i: (i, 0)),
            pl.BlockSpec((tile, 2 * KV_WIDTH), lambda i: (i, 0)),
        ],
        out_shape=[
            jax.ShapeDtypeStruct((n_tok, POOL_WIDTH), F32),
            jax.ShapeDtypeStruct((n_tok, ATTN_WIDTH), BF16),
            jax.ShapeDtypeStruct((n_tok, 2 * KV_WIDTH), BF16),
            jax.ShapeDtypeStruct((n_tok, 2 * KV_WIDTH), BF16),
        ],
        compiler_params=pltpu.CompilerParams(
            dimension_semantics=("arbitrary",),
            vmem_limit_bytes=VMEM_LIMIT_BYTES),
        name="mix_proj_rope" if rope else "mix_proj",
    )(*args)


def _softmax_pv(scores, values, sink_col):
    m = sink_col
    for s in scores:
        m = jnp.maximum(m, jnp.max(s, axis=1, keepdims=True))
    denom = jnp.exp(sink_col - m)
    acc = None
    for s, v in zip(scores, values):
        p = jnp.exp(s - m)
        denom = denom + jnp.sum(p, axis=1, keepdims=True)
        pv = _dot(p.astype(BF16), v)
        acc = pv if acc is None else acc + pv
    return acc * (1.0 / denom)


def _mix_kernel(sink_ref, h_ref, mod_ref, u_ref, uprev_ref, unext_ref, q_ref, *rest,
                tile, seq_len, local):
    if local:
        (kk_ref, kkprev_ref, kknext_ref, vv_ref, vvprev_ref, vvnext_ref,
         kkc_ref, vvc_ref, wpool_ref, pscale_ref, wout_ref, o_ref) = rest
    else:
        kkc_ref, vvc_ref, wpool_ref, pscale_ref, wout_ref, o_ref = rest
    t_id = pl.program_id(1)
    n_tiles = pl.num_programs(1)

    ext_rows = tile + 2 * POOL_HALO
    halo_shape = (POOL_HALO, POOL_WIDTH)
    has_prev = jnp.full(halo_shape, t_id, jnp.int32) > 0
    has_next = jnp.full(halo_shape, t_id, jnp.int32) < n_tiles - 1
    ext = jnp.concatenate([
        jnp.where(has_prev, uprev_ref[...], 0.0),
        u_ref[...],
        jnp.where(has_next, unext_ref[...], 0.0)], axis=0)
    pos = t_id * tile + lax.broadcasted_iota(jnp.int32, (tile, 1), 0)
    pool_parts = []
    for g, w in enumerate(POOL_WINDOWS):
        xg = ext[:, g * POOL_GROUP_DIM:(g + 1) * POOL_GROUP_DIM]
        s = xg
        sh = 1
        while sh < w:
            s = s + pltpu.roll(s, sh, 0)
            sh *= 2
        lead = w // 2 - 1
        if lead:
            s = pltpu.roll(s, ext_rows - lead, 0)
        lo = jnp.maximum(pos - w // 2, 0)
        hi = jnp.minimum(pos + (w - w // 2), seq_len)
        inv_cnt = 1.0 / (hi - lo).astype(F32)
        pooled = s[POOL_HALO:POOL_HALO + tile] * inv_cnt - xg[POOL_HALO:POOL_HALO + tile]
        pool_parts.append(_dot(pooled.astype(BF16), wpool_ref[g]))
    pool_out = jnp.concatenate(pool_parts, axis=1) * pscale_ref[...]

    rows = Q_GROUP * BLOCK
    sink_cols = [
        jnp.concatenate([jnp.full((BLOCK, 1), sink_ref[j + Q_GROUP * a], F32)
                         for j in range(Q_GROUP)], axis=0)
        for a in range(N_KV_HEADS)]
    if local:
        qi = lax.broadcasted_iota(jnp.int32, (rows, BLOCK), 0) % BLOCK
        kj = lax.broadcasted_iota(jnp.int32, (rows, BLOCK), 1)
        tid_v = jnp.full((rows, BLOCK), t_id, jnp.int32)
        prev_ok = kj >= qi
        next_ok = kj <= qi
        prev_ok_first = prev_ok & (tid_v > 0)
        next_ok_last = next_ok & (tid_v < n_tiles - 1)
    n_blk = tile // BLOCK
    attn_blocks = []
    for blk in range(n_blk):
        r0 = blk * BLOCK
        qb = q_ref[r0:r0 + BLOCK, :]
        qs = jnp.concatenate([qb[:, j * LANES:(j + 1) * LANES] for j in range(Q_GROUP)], axis=0)
        if local:
            if blk == 0:
                kp, vp, pmask = kkprev_ref[...], vvprev_ref[...], prev_ok_first
            else:
                kp, vp, pmask = kk_ref[r0 - BLOCK:r0, :], vv_ref[r0 - BLOCK:r0, :], prev_ok
            kc, vc = kk_ref[r0:r0 + BLOCK, :], vv_ref[r0:r0 + BLOCK, :]
            if blk == n_blk - 1:
                kn, vn, nmask = kknext_ref[...], vvnext_ref[...], next_ok_last
            else:
                kn, vn, nmask = (kk_ref[r0 + BLOCK:r0 + 2 * BLOCK, :],
                                 vv_ref[r0 + BLOCK:r0 + 2 * BLOCK, :], next_ok)
        out = None
        for a in range(N_KV_HEADS):
            sl = slice(a * LANES, (a + 1) * LANES)
            scores, values = [], []
            if local:
                scores.append(jnp.where(pmask, _dot_t(qs, kp[:, sl]), NEG_INF))
                values.append(vp[:, sl])
                scores.append(_dot_t(qs, kc[:, sl]))
                values.append(vc[:, sl])
                scores.append(jnp.where(nmask, _dot_t(qs, kn[:, sl]), NEG_INF))
                values.append(vn[:, sl])
            scores.append(_dot_t(qs, kkc_ref[:, sl]))
            values.append(vvc_ref[:, sl])
            o_a = _softmax_pv(scores, values, sink_cols[a])
            out = o_a if out is None else out + o_a
        attn_blocks.append(
            jnp.concatenate([out[j * BLOCK:(j + 1) * BLOCK, :] for j in range(Q_GROUP)], axis=1))
    attn = jnp.concatenate(attn_blocks, axis=0)

    cat = jnp.concatenate([pool_out.astype(BF16), attn.astype(BF16)], axis=1)
    o_ref[...] = h_ref[...] + mod_ref[5:6, :] * _dot(cat, wout_ref[...])


def _mix_call(sink, h, mod, u, q, kk, vv, kkc, vvc, w_pool, pool_scale, w_out, *,
              mod_base, mod_stride, n_batch, seq_len, ctx_len, tile, local):
    n_tok = h.shape[0]
    tiles = seq_len // tile
    halo_per_tile = tile // POOL_HALO
    halo_per_seq = seq_len // POOL_HALO
    blk_per_tile = tile // BLOCK
    blk_per_seq = seq_len // BLOCK

    def main(b, t, s):
        return (b * tiles + t, 0)

    def prev8(b, t, s):
        return (b * halo_per_seq + jnp.maximum(t * halo_per_tile - 1, 0), 0)

    def next8(b, t, s):
        return (b * halo_per_seq + jnp.minimum((t + 1) * halo_per_tile, halo_per_seq - 1), 0)

    def prev_blk(b, t, s):
        return (b * blk_per_seq + jnp.maximum(t * blk_per_tile - 1, 0), 0)

    def next_blk(b, t, s):
        return (b * blk_per_seq + jnp.minimum((t + 1) * blk_per_tile, blk_per_seq - 1), 0)

    def batch(b, t, s):
        return (b, 0)

    def whole(b, t, s):
        return (0, 0)

    in_specs = [
        pl.BlockSpec((tile, D_MODEL), main),
        pl.BlockSpec((None, N_MOD, D_MODEL),
                     lambda b, t, s: (mod_base + mod_stride * b, 0, 0)),
        pl.BlockSpec((tile, POOL_WIDTH), main),
        pl.BlockSpec((POOL_HALO, POOL_WIDTH), prev8),
        pl.BlockSpec((POOL_HALO, POOL_WIDTH), next8),
        pl.BlockSpec((tile, ATTN_WIDTH), main),
    ]
    args = [h, mod, u, u, u, q]
    if local:
        for arr in (kk, vv):
            in_specs += [
                pl.BlockSpec((tile, 2 * KV_WIDTH), main),
                pl.BlockSpec((BLOCK, 2 * KV_WIDTH), prev_blk),
                pl.BlockSpec((BLOCK, 2 * KV_WIDTH), next_blk),
            ]
            args += [arr, arr, arr]
    in_specs += [
        pl.BlockSpec((ctx_len, 2 * KV_WIDTH), batch),
        pl.BlockSpec((ctx_len, 2 * KV_WIDTH), batch),
        pl.BlockSpec((POOL_GROUPS, POOL_GROUP_DIM, POOL_GROUP_DIM), lambda b, t, s: (0, 0, 0)),
        pl.BlockSpec((1, POOL_WIDTH), whole),
        pl.BlockSpec((MIX_WIDTH, D_MODEL), whole),
    ]
    args += [kkc, vvc, w_pool, pool_scale, w_out]
    return pl.pallas_call(
        functools.partial(_mix_kernel, tile=tile, seq_len=seq_len, local=local),
        grid_spec=pltpu.PrefetchScalarGridSpec(
            num_scalar_prefetch=1,
            grid=(n_batch, tiles),
            in_specs=in_specs,
            out_specs=pl.BlockSpec((tile, D_MODEL), main),
        ),
        out_shape=jax.ShapeDtypeStruct((n_tok, D_MODEL), F32),
        compiler_params=pltpu.CompilerParams(
            dimension_semantics=("arbitrary", "arbitrary"),
            vmem_limit_bytes=VMEM_LIMIT_BYTES),
        name="mix_latent" if local else "mix_context",
    )(sink, *args)


def _permute_w_in(w):
    d = w.shape[0]
    half = ROPE_AXIS_DIM
    wq = w[:, POOL_WIDTH:MIX_WIDTH].reshape(d, N_KV_HEADS, Q_GROUP, 2, half)
    wq = wq.transpose(0, 2, 3, 1, 4).reshape(d, ATTN_WIDTH)
    wk = w[:, MIX_WIDTH:MIX_WIDTH + KV_WIDTH].reshape(d, N_KV_HEADS, 2, half)
    wk = wk.transpose(0, 2, 1, 3).reshape(d, KV_WIDTH)
    return jnp.concatenate([w[:, :POOL_WIDTH], wq, wk, w[:, MIX_WIDTH + KV_WIDTH:]], axis=1)


def _permute_w_out(w):
    wa = w[POOL_WIDTH:].reshape(N_KV_HEADS, Q_GROUP, HEAD_DIM, w.shape[1])
    wa = wa.transpose(1, 0, 2, 3).reshape(ATTN_WIDTH, w.shape[1])
    return jnp.concatenate([w[:POOL_WIDTH], wa], axis=0)


def _rope_tables(seq_len):
    rows = seq_len // GRID_W
    row = jnp.repeat(jnp.arange(rows), GRID_W).astype(F32)
    col = jnp.tile(jnp.arange(GRID_W), rows).astype(F32)
    inv = ROPE_BASE ** (-jnp.arange(0, ROPE_AXIS_DIM, 2, dtype=F32) / ROPE_AXIS_DIM)
    ang = jnp.concatenate([row[:, None] * inv, col[:, None] * inv], axis=-1)
    cos, sin = jnp.cos(ang), jnp.sin(ang)
    cos_t = jnp.concatenate([cos, cos, cos, cos], axis=-1)
    sin_t = jnp.concatenate([-sin, -sin, sin, sin], axis=-1)
    return cos_t, sin_t


def kernel(x, c, ctx, c_ctx, w_mod, b_mod, norm_ffn1, w_ffn1_in, w_ffn1_out, norm_mix, w_in,
           w_pool, pool_scale, sink, w_out, norm_ffn2, w_ffn2_in, w_ffn2_out, norm_final):
    n_batch, seq_len, _ = x.shape
    ctx_len = ctx.shape[1]
    depth = w_mod.shape[0]
    assert n_batch + 1 <= MOD_ROWS
    assert seq_len % MIX_TILE == 0 and seq_len % PROJ_TILE == 0 and seq_len % FFN_TILE == 0
    assert (n_batch * ctx_len) % FFN_TILE == 0 and ctx_len % BLOCK == 0

    cos_t, sin_t = _rope_tables(seq_len)
    cond = jnp.concatenate(
        [c, c_ctx[None, :], jnp.zeros((MOD_ROWS - n_batch - 1, D_MODEL), F32)], axis=0)
    mods = _mod_call(cond, w_mod, b_mod).reshape(depth, MOD_ROWS, N_MOD, D_MODEL)

    h = x.reshape(n_batch * seq_len, D_MODEL)
    hc = ctx.reshape(n_batch * ctx_len, D_MODEL)
    lat = dict(mod_base=0, tiles_per_mod=seq_len // FFN_TILE, tile=FFN_TILE)
    con = dict(mod_base=n_batch, tiles_per_mod=(n_batch * ctx_len) // FFN_TILE, tile=FFN_TILE)

    for l in range(depth):
        last = l == depth - 1
        mod = mods[l]
        w1i, w1o = w_ffn1_in[l].astype(BF16), w_ffn1_out[l].astype(BF16)
        w2i, w2o = w_ffn2_in[l].astype(BF16), w_ffn2_out[l].astype(BF16)
        w_in_l = _permute_w_in(w_in[l]).astype(BF16)
        w_out_l = _permute_w_out(w_out[l]).astype(BF16)
        w_pool_l = w_pool[l].astype(BF16)
        pscale = pool_scale[l][None, :]
        g1, gm, g2 = norm_ffn1[l][None, :], norm_mix[l][None, :], norm_ffn2[l][None, :]

        h = _ffn_call(h, mod, g1, w1i, w1o, row0=0, **lat)
        hc = _ffn_call(hc, mod, g1, w1i, w1o, row0=0, **con)

        u, q, kk, vv = _proj_call(h, mod, gm, w_in_l, cos_t, sin_t, mod_base=0,
                                  tiles_per_mod=seq_len // PROJ_TILE, tile=PROJ_TILE, rope=True)
        uc, qc, kkc, vvc = _proj_call(hc, mod, gm, w_in_l, None, None, mod_base=n_batch,
                                      tiles_per_mod=1, tile=n_batch * ctx_len, rope=False)
        h = _mix_call(sink[l], h, mod, u, q, kk, vv, kkc, vvc, w_pool_l, pscale, w_out_l,
                      mod_base=0, mod_stride=1, n_batch=n_batch, seq_len=seq_len, ctx_len=ctx_len,
                      tile=MIX_TILE, local=True)
        if not last:
            hc = _mix_call(sink[l], hc, mod, uc, qc, None, None, kkc, vvc, w_pool_l, pscale,
                           w_out_l, mod_base=n_batch, mod_stride=0, n_batch=n_batch, seq_len=ctx_len,
                           ctx_len=ctx_len, tile=ctx_len, local=False)
            hc = _ffn_call(hc, mod, g2, w2i, w2o, row0=6, **con)
        h = _ffn_call(h, mod, g2, w2i, w2o, row0=6, final_gain=norm_final[None, :] if last else None,
                      **lat)
    return h.reshape(n_batch, seq_len, D_MODEL)
```

```python
import functools

import jax
import jax.numpy as jnp
from jax import lax
from jax.experimental import pallas as pl
from jax.experimental.pallas import tpu as pltpu

D_MODEL = 1024
GRID_W = 64
POOL_WIDTH = 512
POOL_GROUPS = 4
POOL_GROUP_DIM = POOL_WIDTH // POOL_GROUPS
POOL_WINDOWS = (2, 4, 8, 16)
POOL_HALO = 8
N_HEADS = 8
N_KV_HEADS = 2
HEAD_DIM = 64
Q_GROUP = N_HEADS // N_KV_HEADS
ATTN_WIDTH = N_HEADS * HEAD_DIM
KV_WIDTH = N_KV_HEADS * HEAD_DIM
MIX_WIDTH = POOL_WIDTH + ATTN_WIDTH
PROJ_WIDTH = MIX_WIDTH + 2 * KV_WIDTH
BLOCK = 128
ROPE_BASE = 10000.0
ROPE_AXIS_DIM = HEAD_DIM // 2
D_FF = 2816
N_MOD = 9
EPS = 1e-6
NEG_INF = -1e30

LANES = 128
MOD_ROWS = 8
VMEM_LIMIT_BYTES = 60000 * 1024

FFN_TILE = 512
FFN_CHUNK = 256
PROJ_TILE = 1024
MIX_TILE = 512
MOD_COLS = 1024

BF16 = jnp.bfloat16
F32 = jnp.float32


def _dot(a, b):
    return jnp.dot(a, b, preferred_element_type=F32)


def _norm_modulate(x, gain, shift, scale):
    ms = jnp.mean(x * x, axis=-1, keepdims=True)
    return (x * lax.rsqrt(ms + EPS) * gain) * (1.0 + scale) + shift


def _mod_spec(layer, row_of):
    return pl.BlockSpec((None, None, N_MOD, D_MODEL),
                        lambda *idx: (layer, row_of(*idx), 0, 0))


def _mod_kernel(c_ref, w_ref, b_ref, o_ref):
    c = c_ref[...]
    a = (c * jax.nn.sigmoid(c)).astype(BF16)
    o_ref[...] = _dot(a, w_ref[...].astype(BF16)) + b_ref[...]


def _mod_call(cond, w_mod, b_mod):
    depth = w_mod.shape[0]
    n_out = w_mod.shape[2]
    return pl.pallas_call(
        _mod_kernel,
        grid=(depth, n_out // MOD_COLS),
        in_specs=[
            pl.BlockSpec((MOD_ROWS, D_MODEL), lambda l, j: (0, 0)),
            pl.BlockSpec((None, D_MODEL, MOD_COLS), lambda l, j: (l, 0, j)),
            pl.BlockSpec((None, 1, MOD_COLS), lambda l, j: (l, 0, j)),
        ],
        out_specs=pl.BlockSpec((None, MOD_ROWS, MOD_COLS), lambda l, j: (l, 0, j)),
        out_shape=jax.ShapeDtypeStruct((depth, MOD_ROWS, n_out), F32),
        compiler_params=pltpu.CompilerParams(
            dimension_semantics=("arbitrary", "arbitrary"),
            vmem_limit_bytes=VMEM_LIMIT_BYTES),
        name="adaln_mod",
    )(cond, w_mod, b_mod.reshape(depth, 1, n_out))


def _ffn_kernel(*refs, row0, final, n_lat, with_ctx):
    refs = list(refs)
    x_ref = refs.pop(0)
    xc_ref = refs.pop(0) if with_ctx else None
    mod_ref, gain_ref, win_ref, wout_ref = refs[:4]
    refs = refs[4:]
    gfin_ref = refs.pop(0) if final else None
    o_ref = refs.pop(0)
    oc_ref = refs.pop(0) if with_ctx else None
    (act_ref,) = refs

    step = pl.program_id(0)
    x = x_ref[...]
    if with_ctx:
        is_ctx = jnp.full(x.shape, step, jnp.int32) >= n_lat
        x = jnp.where(is_ctx, xc_ref[...], x)
    n = _norm_modulate(x, gain_ref[...], mod_ref[row0:row0 + 1, :],
                       mod_ref[row0 + 1:row0 + 2, :]).astype(BF16)
    for c0 in range(0, D_FF, FFN_CHUNK):
        a = _dot(n, win_ref[:, c0:c0 + FFN_CHUNK])
        b = _dot(n, win_ref[:, D_FF + c0:D_FF + c0 + FFN_CHUNK])
        act_ref[:, c0:c0 + FFN_CHUNK] = (a * jax.nn.sigmoid(a) * b).astype(BF16)
    y = _dot(act_ref[...], wout_ref[...])
    h = x + (0.5 * mod_ref[row0 + 2:row0 + 3, :]) * y
    if final:
        ms = jnp.mean(h * h, axis=-1, keepdims=True)
        h = h * lax.rsqrt(ms + EPS) * gfin_ref[...]
    o_ref[...] = h
    if with_ctx:
        oc_ref[...] = h


def _ffn_call(x, xc, mods, gains, w_in, w_out, *, n_lat_tok, layer, row0, tiles_per_batch,
              ctx_mod_row, tile, final_gain=None):
    n_lat = n_lat_tok // tile
    with_ctx = xc is not None
    n_ctx = xc.shape[0] // tile if with_ctx else 0
    final = final_gain is not None

    def ctx_idx(i):
        return (jnp.maximum(i - n_lat, 0), 0)

    x_spec = pl.BlockSpec((tile, D_MODEL), lambda i: (jnp.minimum(i, n_lat - 1), 0))
    o_spec = pl.BlockSpec((tile, D_MODEL), lambda i: (jnp.minimum(i, n_lat), 0))
    xc_spec = pl.BlockSpec((tile, D_MODEL), ctx_idx)
    in_specs = [x_spec] + ([xc_spec] if with_ctx else []) + [
        _mod_spec(layer, lambda i: jnp.minimum(i // tiles_per_batch, ctx_mod_row)),
        pl.BlockSpec((None, 1, D_MODEL), lambda i: (layer, 0, 0)),
        pl.BlockSpec((None, D_MODEL, 2 * D_FF), lambda i: (layer, 0, 0)),
        pl.BlockSpec((None, D_FF, D_MODEL), lambda i: (layer, 0, 0)),
    ]
    args = [x] + ([xc] if with_ctx else []) + [mods, gains, w_in, w_out]
    if final:
        in_specs.append(pl.BlockSpec((1, D_MODEL), lambda i: (0, 0)))
        args.append(final_gain)
    out_specs = [o_spec] + ([xc_spec] if with_ctx else [])
    out_shape = [jax.ShapeDtypeStruct((n_lat_tok + (tile if with_ctx else 0), D_MODEL), F32)]
    if with_ctx:
        out_shape.append(jax.ShapeDtypeStruct(xc.shape, F32))
    outs = pl.pallas_call(
        functools.partial(_ffn_kernel, row0=row0, final=final, n_lat=n_lat, with_ctx=with_ctx),
        grid=(n_lat + n_ctx,),
        in_specs=in_specs,
        out_specs=out_specs,
        out_shape=out_shape,
        scratch_shapes=[pltpu.VMEM((tile, D_FF), BF16)],
        compiler_params=pltpu.CompilerParams(
            dimension_semantics=("arbitrary",),
            vmem_limit_bytes=VMEM_LIMIT_BYTES),
        name="ffn_final" if final else ("ffn_ctx" if with_ctx else "ffn"),
    )(*args)
    return (outs[0], outs[1]) if with_ctx else (outs[0], None)


def _proj_kernel(x_ref, mod_ref, gain_ref, w_ref, *rest, rope):
    if rope:
        cos_ref, sin_ref, u_ref, q_ref, kt_ref, vv_ref = rest
    else:
        u_ref, q_ref, kt_ref, vv_ref = rest
    n = _norm_modulate(x_ref[...], gain_ref[...], mod_ref[3:4, :], mod_ref[4:5, :]).astype(BF16)
    p = _dot(n, w_ref[...])
    u_ref[...] = p[:, :POOL_WIDTH]
    k = p[:, MIX_WIDTH:MIX_WIDTH + KV_WIDTH]
    v = p[:, MIX_WIDTH + KV_WIDTH:]
    qs = [p[:, POOL_WIDTH + j * LANES:POOL_WIDTH + (j + 1) * LANES] for j in range(Q_GROUP)]
    if rope:
        c = cos_ref[...]
        s = sin_ref[...]
        k = k * c + pltpu.roll(k, LANES // 2, 1) * s
        qs = [q * c + pltpu.roll(q, LANES // 2, 1) * s for q in qs]
    scale = HEAD_DIM ** -0.5
    q_ref[...] = jnp.concatenate([q * scale for q in qs], axis=1).astype(BF16)
    lane = lax.broadcasted_iota(jnp.int32, k.shape, 1)
    kv0_lanes = (lane % (LANES // 2)) < ROPE_AXIS_DIM
    kt_ref[...] = jnp.concatenate(
        [jnp.where(kv0_lanes, k, 0.0).T, jnp.where(kv0_lanes, 0.0, k).T], axis=0).astype(BF16)
    low = lane < HEAD_DIM
    vv_ref[...] = jnp.concatenate(
        [jnp.where(low, v, 0.0), jnp.where(low, 0.0, v)], axis=1).astype(BF16)


def _proj_call(x, mods, gain, w_in, cos_t, sin_t, *, n_tok, layer, mod_base, tiles_per_mod, tile,
               rope):
    in_specs = [
        pl.BlockSpec((tile, D_MODEL), lambda i: (i, 0)),
        _mod_spec(layer, lambda i: mod_base + i // tiles_per_mod),
        pl.BlockSpec((1, D_MODEL), lambda i: (0, 0)),
        pl.BlockSpec((D_MODEL, PROJ_WIDTH), lambda i: (0, 0)),
    ]
    args = [x, mods, gain, w_in]
    if rope:
        in_specs += [pl.BlockSpec((tile, LANES), lambda i: (i % tiles_per_mod, 0))] * 2
        args += [cos_t, sin_t]
    return pl.pallas_call(
        functools.partial(_proj_kernel, rope=rope),
        grid=(n_tok // tile,),
        in_specs=in_specs,
        out_specs=[
            pl.BlockSpec((tile, POOL_WIDTH), lambda i: (i, 0)),
            pl.BlockSpec((tile, ATTN_WIDTH), lambda i: (i, 0)),
            pl.BlockSpec((2 * KV_WIDTH, tile), lambda i: (0, i)),
            pl.BlockSpec((tile, 2 * KV_WIDTH), lambda i: (i, 0)),
        ],
        out_shape=[
            jax.ShapeDtypeStruct((n_tok, POOL_WIDTH), F32),
            jax.ShapeDtypeStruct((n_tok, ATTN_WIDTH), BF16),
            jax.ShapeDtypeStruct((2 * KV_WIDTH, n_tok), BF16),
            jax.ShapeDtypeStruct((n_tok, 2 * KV_WIDTH), BF16),
        ],
        compiler_params=pltpu.CompilerParams(
            dimension_semantics=("arbitrary",),
            vmem_limit_bytes=VMEM_LIMIT_BYTES),
        name="mix_proj_rope" if rope else "mix_proj",
    )(*args)


def _attend(qs, k_t, vv, sink_cols, masks):
    n_slab = len(masks)
    n_keys = n_slab * BLOCK
    s = _dot(qs, k_t)
    probs, inv_denoms = [], []
    for a in range(N_KV_HEADS):
        slabs = []
        for i, mask in enumerate(masks):
            c0 = a * n_keys + i * BLOCK
            piece = s[:, c0:c0 + BLOCK]
            slabs.append(piece if mask is None else jnp.where(mask, piece, NEG_INF))
        m_el = slabs[0]
        for piece in slabs[1:]:
            m_el = jnp.maximum(m_el, piece)
        m = jnp.maximum(jnp.max(m_el, axis=1, keepdims=True), sink_cols[a])
        ps = [jnp.exp(piece - m) for piece in slabs]
        l_el = ps[0]
        for piece in ps[1:]:
            l_el = l_el + piece
        denom = jnp.sum(l_el, axis=1, keepdims=True) + jnp.exp(sink_cols[a] - m)
        inv_denoms.append(1.0 / denom)
        probs += [piece.astype(BF16) for piece in ps]
    o = _dot(jnp.concatenate(probs, axis=1), vv)
    lane = lax.broadcasted_iota(jnp.int32, o.shape, 1)
    return o * jnp.where(lane < HEAD_DIM, inv_denoms[0], inv_denoms[1])


def _mix_kernel(sink_ref, h_ref, mod_ref, u_ref, uprev_ref, unext_ref, q_ref, *rest,
                tile, seq_len, local):
    if local:
        (kt_ref, ktprev_ref, ktnext_ref, vv_ref, vvprev_ref, vvnext_ref,
         ktc_ref, vvc_ref, wpool_ref, pscale_ref, wout_ref, o_ref) = rest
    else:
        ktc_ref, vvc_ref, wpool_ref, pscale_ref, wout_ref, o_ref = rest
    t_id = pl.program_id(1)
    n_tiles = pl.num_programs(1)

    ext_rows = tile + 2 * POOL_HALO
    halo_shape = (POOL_HALO, POOL_WIDTH)
    has_prev = jnp.full(halo_shape, t_id, jnp.int32) > 0
    has_next = jnp.full(halo_shape, t_id, jnp.int32) < n_tiles - 1
    ext = jnp.concatenate([
        jnp.where(has_prev, uprev_ref[...], 0.0),
        u_ref[...],
        jnp.where(has_next, unext_ref[...], 0.0)], axis=0)
    pos = t_id * tile + lax.broadcasted_iota(jnp.int32, (tile, 1), 0)
    pool_parts = []
    for g, w in enumerate(POOL_WINDOWS):
        xg = ext[:, g * POOL_GROUP_DIM:(g + 1) * POOL_GROUP_DIM]
        s = xg
        sh = 1
        while sh < w:
            s = s + pltpu.roll(s, sh, 0)
            sh *= 2
        lead = w // 2 - 1
        if lead:
            s = pltpu.roll(s, ext_rows - lead, 0)
        lo = jnp.maximum(pos - w // 2, 0)
        hi = jnp.minimum(pos + (w - w // 2), seq_len)
        inv_cnt = 1.0 / (hi - lo).astype(F32)
        pooled = s[POOL_HALO:POOL_HALO + tile] * inv_cnt - xg[POOL_HALO:POOL_HALO + tile]
        pool_parts.append(_dot(pooled.astype(BF16), wpool_ref[g]))
    pool_out = jnp.concatenate(pool_parts, axis=1) * pscale_ref[...]

    rows = Q_GROUP * BLOCK
    sink_cols = [
        jnp.concatenate([jnp.full((BLOCK, 1), sink_ref[j + Q_GROUP * a], F32)
                         for j in range(Q_GROUP)], axis=0)
        for a in range(N_KV_HEADS)]
    n_ctx_slab = ktc_ref.shape[1] // BLOCK
    if local:
        qi = lax.broadcasted_iota(jnp.int32, (rows, BLOCK), 0) % BLOCK
        kj = lax.broadcasted_iota(jnp.int32, (rows, BLOCK), 1)
        tid_v = jnp.full((rows, BLOCK), t_id, jnp.int32)
        prev_ok = kj >= qi
        next_ok = kj <= qi
        prev_ok_first = prev_ok & (tid_v > 0)
        next_ok_last = next_ok & (tid_v < n_tiles - 1)
    n_blk = tile // BLOCK
    attn_blocks = []
    for blk in range(n_blk):
        r0 = blk * BLOCK
        qb = q_ref[r0:r0 + BLOCK, :]
        qs = jnp.concatenate([qb[:, j * LANES:(j + 1) * LANES] for j in range(Q_GROUP)], axis=0)
        kt_parts, vv_parts = [], []
        for a in range(N_KV_HEADS):
            ra = slice(a * LANES, (a + 1) * LANES)
            if local:
                if blk == 0:
                    kt_parts.append(ktprev_ref[ra, :])
                    vv_parts.append(vvprev_ref[:, ra])
                else:
                    kt_parts.append(kt_ref[ra, r0 - BLOCK:r0])
                    vv_parts.append(vv_ref[r0 - BLOCK:r0, ra])
                kt_parts.append(kt_ref[ra, r0:r0 + BLOCK])
                vv_parts.append(vv_ref[r0:r0 + BLOCK, ra])
                if blk == n_blk - 1:
                    kt_parts.append(ktnext_ref[ra, :])
                    vv_parts.append(vvnext_ref[:, ra])
                else:
                    kt_parts.append(kt_ref[ra, r0 + BLOCK:r0 + 2 * BLOCK])
                    vv_parts.append(vv_ref[r0 + BLOCK:r0 + 2 * BLOCK, ra])
            kt_parts.append(ktc_ref[ra, :])
            vv_parts.append(vvc_ref[:, ra])
        masks = [None] * n_ctx_slab
        if local:
            masks = [prev_ok_first if blk == 0 else prev_ok, None,
                     next_ok_last if blk == n_blk - 1 else next_ok] + masks
        out = _attend(qs, jnp.concatenate(kt_parts, axis=1), jnp.concatenate(vv_parts, axis=0),
                      sink_cols, masks)
        attn_blocks.append(
            jnp.concatenate([out[j * BLOCK:(j + 1) * BLOCK, :] for j in range(Q_GROUP)], axis=1))
    attn = jnp.concatenate(attn_blocks, axis=0)

    cat = jnp.concatenate([pool_out.astype(BF16), attn.astype(BF16)], axis=1)
    o_ref[...] = h_ref[...] + mod_ref[5:6, :] * _dot(cat, wout_ref[...])


def _mix_call(sink, h, mods, u, q, kt, vv, ktc, vvc, w_pool, pool_scale, w_out, *,
              layer, mod_base, mod_stride, n_batch, seq_len, ctx_len, tile, local):
    n_tok = n_batch * seq_len
    tiles = seq_len // tile
    halo_per_tile = tile // POOL_HALO
    halo_per_seq = seq_len // POOL_HALO
    blk_per_tile = tile // BLOCK
    blk_per_seq = seq_len // BLOCK

    def main(b, t, s):
        return (b * tiles + t, 0)

    def prev8(b, t, s):
        return (b * halo_per_seq + jnp.maximum(t * halo_per_tile - 1, 0), 0)

    def next8(b, t, s):
        return (b * halo_per_seq + jnp.minimum((t + 1) * halo_per_tile, halo_per_seq - 1), 0)

    def prev_blk(b, t, s):
        return (b * blk_per_seq + jnp.maximum(t * blk_per_tile - 1, 0), 0)

    def next_blk(b, t, s):
        return (b * blk_per_seq + jnp.minimum((t + 1) * blk_per_tile, blk_per_seq - 1), 0)

    def batch(b, t, s):
        return (b, 0)

    def whole(b, t, s):
        return (0, 0)

    def flip(index_map):
        return lambda b, t, s: index_map(b, t, s)[::-1]

    in_specs = [
        pl.BlockSpec((tile, D_MODEL), main),
        _mod_spec(layer, lambda b, t, s: mod_base + mod_stride * b),
        pl.BlockSpec((tile, POOL_WIDTH), main),
        pl.BlockSpec((POOL_HALO, POOL_WIDTH), prev8),
        pl.BlockSpec((POOL_HALO, POOL_WIDTH), next8),
        pl.BlockSpec((tile, ATTN_WIDTH), main),
    ]
    args = [h, mods, u, u, u, q]
    if local:
        in_specs += [
            pl.BlockSpec((2 * KV_WIDTH, tile), flip(main)),
            pl.BlockSpec((2 * KV_WIDTH, BLOCK), flip(prev_blk)),
            pl.BlockSpec((2 * KV_WIDTH, BLOCK), flip(next_blk)),
            pl.BlockSpec((tile, 2 * KV_WIDTH), main),
            pl.BlockSpec((BLOCK, 2 * KV_WIDTH), prev_blk),
            pl.BlockSpec((BLOCK, 2 * KV_WIDTH), next_blk),
        ]
        args += [kt, kt, kt, vv, vv, vv]
    in_specs += [
        pl.BlockSpec((2 * KV_WIDTH, ctx_len), flip(batch)),
        pl.BlockSpec((ctx_len, 2 * KV_WIDTH), batch),
        pl.BlockSpec((POOL_GROUPS, POOL_GROUP_DIM, POOL_GROUP_DIM), lambda b, t, s: (0, 0, 0)),
        pl.BlockSpec((1, POOL_WIDTH), whole),
        pl.BlockSpec((MIX_WIDTH, D_MODEL), whole),
    ]
    args += [ktc, vvc, w_pool, pool_scale, w_out]
    return pl.pallas_call(
        functools.partial(_mix_kernel, tile=tile, seq_len=seq_len, local=local),
        grid_spec=pltpu.PrefetchScalarGridSpec(
            num_scalar_prefetch=1,
            grid=(n_batch, tiles),
            in_specs=in_specs,
            out_specs=pl.BlockSpec((tile, D_MODEL), main),
        ),
        out_shape=jax.ShapeDtypeStruct((n_tok, D_MODEL), F32),
        compiler_params=pltpu.CompilerParams(
            dimension_semantics=("arbitrary", "arbitrary"),
            vmem_limit_bytes=VMEM_LIMIT_BYTES),
        name="mix_latent" if local else "mix_context",
    )(sink, *args)


def _permute_w_in(w):
    d = w.shape[0]
    half = ROPE_AXIS_DIM
    wq = w[:, POOL_WIDTH:MIX_WIDTH].reshape(d, N_KV_HEADS, Q_GROUP, 2, half)
    wq = wq.transpose(0, 2, 3, 1, 4).reshape(d, ATTN_WIDTH)
    wk = w[:, MIX_WIDTH:MIX_WIDTH + KV_WIDTH].reshape(d, N_KV_HEADS, 2, half)
    wk = wk.transpose(0, 2, 1, 3).reshape(d, KV_WIDTH)
    return jnp.concatenate([w[:, :POOL_WIDTH], wq, wk, w[:, MIX_WIDTH + KV_WIDTH:]], axis=1)


def _permute_w_out(w):
    wa = w[POOL_WIDTH:].reshape(N_KV_HEADS, Q_GROUP, HEAD_DIM, w.shape[1])
    wa = wa.transpose(1, 0, 2, 3).reshape(ATTN_WIDTH, w.shape[1])
    return jnp.concatenate([w[:POOL_WIDTH], wa], axis=0)


def _rope_tables(seq_len):
    rows = seq_len // GRID_W
    row = jnp.repeat(jnp.arange(rows), GRID_W).astype(F32)
    col = jnp.tile(jnp.arange(GRID_W), rows).astype(F32)
    inv = ROPE_BASE ** (-jnp.arange(0, ROPE_AXIS_DIM, 2, dtype=F32) / ROPE_AXIS_DIM)
    ang = jnp.concatenate([row[:, None] * inv, col[:, None] * inv], axis=-1)
    cos, sin = jnp.cos(ang), jnp.sin(ang)
    cos_t = jnp.concatenate([cos, cos, cos, cos], axis=-1)
    sin_t = jnp.concatenate([-sin, -sin, sin, sin], axis=-1)
    return cos_t, sin_t


def kernel(x, c, ctx, c_ctx, w_mod, b_mod, norm_ffn1, w_ffn1_in, w_ffn1_out, norm_mix, w_in,
           w_pool, pool_scale, sink, w_out, norm_ffn2, w_ffn2_in, w_ffn2_out, norm_final):
    n_batch, seq_len, _ = x.shape
    ctx_len = ctx.shape[1]
    depth = w_mod.shape[0]
    assert n_batch + 1 <= MOD_ROWS
    assert seq_len % MIX_TILE == 0 and seq_len % PROJ_TILE == 0 and seq_len % FFN_TILE == 0
    assert (n_batch * ctx_len) % FFN_TILE == 0 and ctx_len % BLOCK == 0

    cos_t, sin_t = _rope_tables(seq_len)
    cond = jnp.concatenate(
        [c, c_ctx[None, :], jnp.zeros((MOD_ROWS - n_batch - 1, D_MODEL), F32)], axis=0)
    mods = _mod_call(cond, w_mod, b_mod).reshape(depth, MOD_ROWS, N_MOD, D_MODEL)

    h = x.reshape(n_batch * seq_len, D_MODEL)
    hc = ctx.reshape(n_batch * ctx_len, D_MODEL)
    ffn_cfg = dict(n_lat_tok=n_batch * seq_len, tiles_per_batch=seq_len // FFN_TILE,
                   ctx_mod_row=n_batch, tile=FFN_TILE)
    w1i, w1o = w_ffn1_in.astype(BF16), w_ffn1_out.astype(BF16)
    w2i, w2o = w_ffn2_in.astype(BF16), w_ffn2_out.astype(BF16)
    g1, g2 = norm_ffn1[:, None, :], norm_ffn2[:, None, :]

    for l in range(depth):
        last = l == depth - 1
        w_in_l = _permute_w_in(w_in[l]).astype(BF16)
        w_out_l = _permute_w_out(w_out[l]).astype(BF16)
        w_pool_l = w_pool[l].astype(BF16)
        pscale = pool_scale[l][None, :]
        gm = norm_mix[l][None, :]

        h, hc = _ffn_call(h, hc, mods, g1, w1i, w1o, layer=l, row0=0, **ffn_cfg)

        u, q, kt, vv = _proj_call(h, mods, gm, w_in_l, cos_t, sin_t, n_tok=n_batch * seq_len,
                                  layer=l, mod_base=0, tiles_per_mod=seq_len // PROJ_TILE,
                                  tile=PROJ_TILE, rope=True)
        uc, qc, ktc, vvc = _proj_call(hc, mods, gm, w_in_l, None, None, n_tok=n_batch * ctx_len,
                                      layer=l, mod_base=n_batch, tiles_per_mod=1,
                                      tile=n_batch * ctx_len, rope=False)
        h = _mix_call(sink[l], h, mods, u, q, kt, vv, ktc, vvc, w_pool_l, pscale, w_out_l,
                      layer=l, mod_base=0, mod_stride=1, n_batch=n_batch, seq_len=seq_len,
                      ctx_len=ctx_len, tile=MIX_TILE, local=True)
        if not last:
            hc = _mix_call(sink[l], hc, mods, uc, qc, None, None, ktc, vvc, w_pool_l, pscale,
                           w_out_l, layer=l, mod_base=n_batch, mod_stride=0, n_batch=n_batch,
                           seq_len=ctx_len, ctx_len=ctx_len, tile=ctx_len, local=False)
            h, hc = _ffn_call(h, hc, mods, g2, w2i, w2o, layer=l, row0=6, **ffn_cfg)
        else:
            h, _ = _ffn_call(h, None, mods, g2, w2i, w2o, layer=l, row0=6,
                             final_gain=norm_final[None, :], **ffn_cfg)
    return h.reshape(n_batch, seq_len, D_MODEL)
```

```python
import functools

import jax
import jax.numpy as jnp
import numpy as np
from jax import lax
from jax.experimental import pallas as pl
from jax.experimental.pallas import tpu as pltpu

D_MODEL = 1024
GRID_W = 64
POOL_WIDTH = 512
POOL_GROUPS = 4
POOL_GROUP_DIM = POOL_WIDTH // POOL_GROUPS
POOL_WINDOWS = (2, 4, 8, 16)
POOL_HALO = 8
N_HEADS = 8
N_KV_HEADS = 2
HEAD_DIM = 64
Q_GROUP = N_HEADS // N_KV_HEADS
ATTN_WIDTH = N_HEADS * HEAD_DIM
KV_WIDTH = N_KV_HEADS * HEAD_DIM
MIX_WIDTH = POOL_WIDTH + ATTN_WIDTH
PROJ_WIDTH = MIX_WIDTH + 2 * KV_WIDTH
BLOCK = 128
ROPE_BASE = 10000.0
ROPE_AXIS_DIM = HEAD_DIM // 2
D_FF = 2816
N_MOD = 9
EPS = 1e-6
NEG_INF = -1e30
LOG2_E = 1.4426950408889634

LANES = 128
BF16_SUBLANES = 16
MOD_ROWS = 8
VMEM_LIMIT_BYTES = 60000 * 1024

FFN_TILE = 1024
FFN_CHUNK = 256
PROJ_TILE = 1024
MIX_TILE = 512
MOD_COLS = 1024

BF16 = jnp.bfloat16
F32 = jnp.float32


def _dot(a, b):
    return jnp.dot(a, b, preferred_element_type=F32)


def _norm_modulate(x, gain, shift, scale):
    ms = jnp.mean(x * x, axis=-1, keepdims=True)
    return (x * lax.rsqrt(ms + EPS) * gain) * (1.0 + scale) + shift


def _mod_spec(layer, row_of):
    return pl.BlockSpec((None, None, N_MOD, D_MODEL),
                        lambda *idx: (layer, row_of(*idx), 0, 0))


def _mod_kernel(c_ref, w_ref, b_ref, o_ref):
    c = c_ref[...]
    a = (c * jax.nn.sigmoid(c)).astype(BF16)
    o_ref[...] = _dot(a, w_ref[...].astype(BF16)) + b_ref[...]


def _mod_call(cond, w_mod, b_mod):
    depth = w_mod.shape[0]
    n_out = w_mod.shape[2]
    return pl.pallas_call(
        _mod_kernel,
        grid=(depth, n_out // MOD_COLS),
        in_specs=[
            pl.BlockSpec((MOD_ROWS, D_MODEL), lambda l, j: (0, 0)),
            pl.BlockSpec((None, D_MODEL, MOD_COLS), lambda l, j: (l, 0, j)),
            pl.BlockSpec((None, 1, MOD_COLS), lambda l, j: (l, 0, j)),
        ],
        out_specs=pl.BlockSpec((None, MOD_ROWS, MOD_COLS), lambda l, j: (l, 0, j)),
        out_shape=jax.ShapeDtypeStruct((depth, MOD_ROWS, n_out), F32),
        compiler_params=pltpu.CompilerParams(
            dimension_semantics=("arbitrary", "arbitrary"),
            vmem_limit_bytes=VMEM_LIMIT_BYTES),
        name="adaln_mod",
    )(cond, w_mod, b_mod.reshape(depth, 1, n_out))


def _ffn_kernel(*refs, row0, final, n_lat, with_ctx, n_cast):
    refs = list(refs)
    x_ref = refs.pop(0)
    xc_ref = refs.pop(0) if with_ctx else None
    mod_ref, gain_ref, win_ref, wout_ref = refs[:4]
    refs = refs[4:]
    gfin_ref = refs.pop(0) if final else None
    cast_srcs, refs = refs[:n_cast], refs[n_cast:]
    o_ref = refs.pop(0)
    oc_ref = refs.pop(0) if with_ctx else None
    cast_dsts, refs = refs[:n_cast], refs[n_cast:]
    (act_ref,) = refs

    for src, dst in zip(cast_srcs, cast_dsts):
        dst[...] = src[...].astype(BF16)

    step = pl.program_id(0)
    x = x_ref[...]
    if with_ctx:
        is_ctx = jnp.full(x.shape, step, jnp.int32) >= n_lat
        x = jnp.where(is_ctx, xc_ref[...], x)
    n = _norm_modulate(x, gain_ref[...], mod_ref[row0:row0 + 1, :],
                       mod_ref[row0 + 1:row0 + 2, :]).astype(BF16)
    for c0 in range(0, D_FF, FFN_CHUNK):
        a = _dot(n, win_ref[:, c0:c0 + FFN_CHUNK])
        b = _dot(n, win_ref[:, D_FF + c0:D_FF + c0 + FFN_CHUNK])
        act_ref[:, c0:c0 + FFN_CHUNK] = (a * jax.nn.sigmoid(a) * b).astype(BF16)
    y = _dot(act_ref[...], wout_ref[...])
    h = x + (0.5 * mod_ref[row0 + 2:row0 + 3, :]) * y
    if final:
        ms = jnp.mean(h * h, axis=-1, keepdims=True)
        h = h * lax.rsqrt(ms + EPS) * gfin_ref[...]
    o_ref[...] = h
    if with_ctx:
        oc_ref[...] = h


def _ffn_call(x, xc, mods, gain, w_in, w_out, *, n_lat_tok, layer, row0, tiles_per_batch,
              ctx_mod_row, tile, final_gain=None, cast=()):
    n_lat = n_lat_tok // tile
    with_ctx = xc is not None
    n_ctx = xc.shape[0] // tile if with_ctx else 0
    final = final_gain is not None

    def ctx_idx(i):
        return (jnp.maximum(i - n_lat, 0), 0)

    def lat_clamped(i):
        return jnp.minimum(i, n_lat - 1)

    x_spec = pl.BlockSpec((tile, D_MODEL), lambda i: (lat_clamped(i), 0))
    o_spec = pl.BlockSpec((tile, D_MODEL), lambda i: (jnp.minimum(i, n_lat), 0))
    xc_spec = pl.BlockSpec((tile, D_MODEL), ctx_idx, pipeline_mode=pl.Buffered(1))
    in_specs = [x_spec] + ([xc_spec] if with_ctx else []) + [
        _mod_spec(layer, lambda i: jnp.minimum(i // tiles_per_batch, ctx_mod_row)),
        pl.BlockSpec((1, D_MODEL), lambda i: (0, 0)),
        pl.BlockSpec((D_MODEL, 2 * D_FF), lambda i: (0, 0)),
        pl.BlockSpec((D_FF, D_MODEL), lambda i: (0, 0)),
    ]
    args = [x] + ([xc] if with_ctx else []) + [mods, gain, w_in, w_out]
    if final:
        in_specs.append(pl.BlockSpec((1, D_MODEL), lambda i: (0, 0)))
        args.append(final_gain)
    out_specs = [o_spec] + ([xc_spec] if with_ctx else [])
    out_shape = [jax.ShapeDtypeStruct((n_lat_tok + (tile if with_ctx else 0), D_MODEL), F32)]
    if with_ctx:
        out_shape.append(jax.ShapeDtypeStruct(xc.shape, F32))
    for w_stack, w_layer in cast:
        _, n_rows, n_cols = w_stack.shape
        slab = n_rows // n_lat
        assert slab * n_lat == n_rows and slab % BF16_SUBLANES == 0
        in_specs.append(pl.BlockSpec((None, slab, n_cols),
                                     lambda i, w_layer=w_layer: (w_layer, lat_clamped(i), 0)))
        args.append(w_stack)
        out_specs.append(pl.BlockSpec((slab, n_cols), lambda i: (lat_clamped(i), 0)))
        out_shape.append(jax.ShapeDtypeStruct((n_rows, n_cols), BF16))
    outs = pl.pallas_call(
        functools.partial(_ffn_kernel, row0=row0, final=final, n_lat=n_lat, with_ctx=with_ctx,
                          n_cast=len(cast)),
        grid=(n_lat + n_ctx,),
        in_specs=in_specs,
        out_specs=out_specs,
        out_shape=out_shape,
        scratch_shapes=[pltpu.VMEM((tile, D_FF), BF16)],
        compiler_params=pltpu.CompilerParams(
            dimension_semantics=("arbitrary",),
            vmem_limit_bytes=VMEM_LIMIT_BYTES),
        name="ffn_final" if final else ("ffn_ctx" if with_ctx else "ffn"),
    )(*args)
    n_tok_outs = 2 if with_ctx else 1
    return outs[0], (outs[1] if with_ctx else None), list(outs[n_tok_outs:])


def _proj_kernel(x_ref, mod_ref, gain_ref, w_ref, *rest, rope):
    if rope:
        cos_ref, sin_ref, u_ref, q_ref, kt_ref, vv_ref = rest
    else:
        u_ref, q_ref, kt_ref, vv_ref = rest
    n = _norm_modulate(x_ref[...], gain_ref[...], mod_ref[3:4, :], mod_ref[4:5, :]).astype(BF16)
    p = _dot(n, w_ref[...])
    u_ref[...] = p[:, :POOL_WIDTH]
    k = p[:, MIX_WIDTH:MIX_WIDTH + KV_WIDTH]
    v = p[:, MIX_WIDTH + KV_WIDTH:]
    qs = [p[:, POOL_WIDTH + j * LANES:POOL_WIDTH + (j + 1) * LANES] for j in range(Q_GROUP)]
    if rope:
        c = cos_ref[...]
        s = sin_ref[...]
        k = k * c + pltpu.roll(k, LANES // 2, 1) * s
        qs = [q * c + pltpu.roll(q, LANES // 2, 1) * s for q in qs]
    scale = HEAD_DIM ** -0.5 * LOG2_E
    q_ref[...] = jnp.concatenate([q * scale for q in qs], axis=1).astype(BF16)
    lane = lax.broadcasted_iota(jnp.int32, k.shape, 1)
    kv0_lanes = (lane % (LANES // 2)) < ROPE_AXIS_DIM
    kt_ref[...] = jnp.concatenate(
        [jnp.where(kv0_lanes, k, 0.0).T, jnp.where(kv0_lanes, 0.0, k).T], axis=0).astype(BF16)
    low = lane < HEAD_DIM
    vv_ref[...] = jnp.concatenate(
        [jnp.where(low, v, 0.0), jnp.where(low, 0.0, v)], axis=1).astype(BF16)


def _proj_call(x, mods, gain, w_in, cos_t, sin_t, *, n_tok, layer, mod_base, tiles_per_mod, tile,
               rope):
    in_specs = [
        pl.BlockSpec((tile, D_MODEL), lambda i: (i, 0)),
        _mod_spec(layer, lambda i: mod_base + i // tiles_per_mod),
        pl.BlockSpec((1, D_MODEL), lambda i: (0, 0)),
        pl.BlockSpec((D_MODEL, PROJ_WIDTH), lambda i: (0, 0)),
    ]
    args = [x, mods, gain, w_in]
    if rope:
        in_specs += [pl.BlockSpec((tile, LANES), lambda i: (i % tiles_per_mod, 0))] * 2
        args += [cos_t, sin_t]
    return pl.pallas_call(
        functools.partial(_proj_kernel, rope=rope),
        grid=(n_tok // tile,),
        in_specs=in_specs,
        out_specs=[
            pl.BlockSpec((tile, POOL_WIDTH), lambda i: (i, 0)),
            pl.BlockSpec((tile, ATTN_WIDTH), lambda i: (i, 0)),
            pl.BlockSpec((2 * KV_WIDTH, tile), lambda i: (0, i)),
            pl.BlockSpec((tile, 2 * KV_WIDTH), lambda i: (i, 0)),
        ],
        out_shape=[
            jax.ShapeDtypeStruct((n_tok, POOL_WIDTH), F32),
            jax.ShapeDtypeStruct((n_tok, ATTN_WIDTH), BF16),
            jax.ShapeDtypeStruct((2 * KV_WIDTH, n_tok), BF16),
            jax.ShapeDtypeStruct((n_tok, 2 * KV_WIDTH), BF16),
        ],
        compiler_params=pltpu.CompilerParams(
            dimension_semantics=("arbitrary",),
            vmem_limit_bytes=VMEM_LIMIT_BYTES),
        name="mix_proj_rope" if rope else "mix_proj",
    )(*args)


def _softmax_pv(s, vv, sink_cols, masks):
    n_keys = len(masks) * BLOCK
    probs, inv_denoms = [], []
    for a in range(N_KV_HEADS):
        slabs = []
        for i, mask in enumerate(masks):
            c0 = a * n_keys + i * BLOCK
            piece = s[:, c0:c0 + BLOCK]
            slabs.append(piece if mask is None else jnp.where(mask, piece, NEG_INF))
        m_el = slabs[0]
        for piece in slabs[1:]:
            m_el = jnp.maximum(m_el, piece)
        m = jnp.maximum(jnp.max(m_el, axis=1, keepdims=True), sink_cols[a])
        ps = [jnp.exp2(piece - m) for piece in slabs]
        l_el = ps[0]
        for piece in ps[1:]:
            l_el = l_el + piece
        denom = jnp.sum(l_el, axis=1, keepdims=True) + jnp.exp2(sink_cols[a] - m)
        inv_denoms.append(1.0 / denom)
        probs += [piece.astype(BF16) for piece in ps]
    o = _dot(jnp.concatenate(probs, axis=1), vv)
    lane = lax.broadcasted_iota(jnp.int32, o.shape, 1)
    return o * jnp.where(lane < HEAD_DIM, inv_denoms[0], inv_denoms[1])


def _mix_kernel(sink_ref, h_ref, mod_ref, u_ref, uprev_ref, unext_ref, q_ref, *rest,
                tile, seq_len, local):
    if local:
        (kt_ref, ktprev_ref, ktnext_ref, vv_ref, vvprev_ref, vvnext_ref,
         ktc_ref, vvc_ref, wpool_ref, pscale_ref, wout_ref, o_ref) = rest
    else:
        ktc_ref, vvc_ref, wpool_ref, pscale_ref, wout_ref, o_ref = rest
    t_id = pl.program_id(1)
    n_tiles = pl.num_programs(1)
    n_blk = tile // BLOCK

    ext_rows = tile + 2 * POOL_HALO
    halo_shape = (POOL_HALO, POOL_GROUP_DIM)
    has_prev = jnp.full(halo_shape, t_id, jnp.int32) > 0
    has_next = jnp.full(halo_shape, t_id, jnp.int32) < n_tiles - 1
    pos = t_id * tile + lax.broadcasted_iota(jnp.int32, (tile, 1), 0)

    def pool_group(g):
        w = POOL_WINDOWS[g]
        lanes = slice(g * POOL_GROUP_DIM, (g + 1) * POOL_GROUP_DIM)
        x = u_ref[:, lanes]
        xg = jnp.concatenate([jnp.where(has_prev, uprev_ref[:, lanes], 0.0), x,
                              jnp.where(has_next, unext_ref[:, lanes], 0.0)], axis=0)
        s = xg
        k = 1
        while k < w // 2:
            s = s + pltpu.roll(s, ext_rows - k, 0)
            k *= 2
        s = s + pltpu.roll(s, w // 2, 0)
        lo = jnp.maximum(pos - w // 2, 0)
        hi = jnp.minimum(pos + (w - w // 2), seq_len)
        inv_cnt = 1.0 / (hi - lo).astype(F32)
        pooled = s[POOL_HALO:POOL_HALO + tile] * inv_cnt - x
        return _dot(pooled.astype(BF16), wpool_ref[g]) * pscale_ref[:, lanes]

    rows = Q_GROUP * BLOCK
    sink_cols = [
        jnp.concatenate([jnp.full((BLOCK, 1), sink_ref[j + Q_GROUP * a] * LOG2_E, F32)
                         for j in range(Q_GROUP)], axis=0)
        for a in range(N_KV_HEADS)]
    n_ctx_slab = ktc_ref.shape[1] // BLOCK
    if local:
        qi = lax.broadcasted_iota(jnp.int32, (rows, BLOCK), 0) % BLOCK
        kj = lax.broadcasted_iota(jnp.int32, (rows, BLOCK), 1)
        tid_v = jnp.full((rows, BLOCK), t_id, jnp.int32)
        prev_ok = kj >= qi
        next_ok = kj <= qi
        prev_ok_first = prev_ok & (tid_v > 0)
        next_ok_last = next_ok & (tid_v < n_tiles - 1)

    def scores(blk):
        r0 = blk * BLOCK
        qb = q_ref[r0:r0 + BLOCK, :]
        qs = jnp.concatenate([qb[:, j * LANES:(j + 1) * LANES] for j in range(Q_GROUP)], axis=0)
        kt_parts = []
        for a in range(N_KV_HEADS):
            ra = slice(a * LANES, (a + 1) * LANES)
            if local:
                kt_parts += [
                    ktprev_ref[ra, :] if blk == 0 else kt_ref[ra, r0 - BLOCK:r0],
                    kt_ref[ra, r0:r0 + BLOCK],
                    ktnext_ref[ra, :] if blk == n_blk - 1 else kt_ref[ra, r0 + BLOCK:r0 + 2 * BLOCK]]
            kt_parts.append(ktc_ref[ra, :])
        return _dot(qs, jnp.concatenate(kt_parts, axis=1))

    def attend(blk, s):
        r0 = blk * BLOCK
        vv_parts = []
        for a in range(N_KV_HEADS):
            ra = slice(a * LANES, (a + 1) * LANES)
            if local:
                vv_parts += [
                    vvprev_ref[:, ra] if blk == 0 else vv_ref[r0 - BLOCK:r0, ra],
                    vv_ref[r0:r0 + BLOCK, ra],
                    vvnext_ref[:, ra] if blk == n_blk - 1 else vv_ref[r0 + BLOCK:r0 + 2 * BLOCK, ra]]
            vv_parts.append(vvc_ref[:, ra])
        masks = [None] * n_ctx_slab
        if local:
            masks = [prev_ok_first if blk == 0 else prev_ok, None,
                     next_ok_last if blk == n_blk - 1 else next_ok] + masks
        out = _softmax_pv(s, jnp.concatenate(vv_parts, axis=0), sink_cols, masks)
        return jnp.concatenate([out[j * BLOCK:(j + 1) * BLOCK, :] for j in range(Q_GROUP)], axis=1)

    attn_blocks, pool_parts = [], []
    s_next = scores(0)
    for blk in range(n_blk):
        s_cur = s_next
        if blk + 1 < n_blk:
            s_next = scores(blk + 1)
        attn_blocks.append(attend(blk, s_cur))
    for g in range(POOL_GROUPS):
        pool_parts.append(pool_group(g))
    attn = jnp.concatenate(attn_blocks, axis=0)
    pool_out = jnp.concatenate(pool_parts, axis=1)

    cat = jnp.concatenate([pool_out.astype(BF16), attn.astype(BF16)], axis=1)
    o_ref[...] = h_ref[...] + mod_ref[5:6, :] * _dot(cat, wout_ref[...])


def _mix_call(sink, h, mods, u, q, kt, vv, ktc, vvc, w_pool, pool_scale, w_out, *,
              layer, mod_base, mod_stride, n_batch, seq_len, ctx_len, tile, local):
    n_tok = n_batch * seq_len
    tiles = seq_len // tile
    halo_per_tile = tile // POOL_HALO
    halo_per_seq = seq_len // POOL_HALO
    blk_per_tile = tile // BLOCK
    blk_per_seq = seq_len // BLOCK

    def main(b, t, s):
        return (b * tiles + t, 0)

    def prev8(b, t, s):
        return (b * halo_per_seq + jnp.maximum(t * halo_per_tile - 1, 0), 0)

    def next8(b, t, s):
        return (b * halo_per_seq + jnp.minimum((t + 1) * halo_per_tile, halo_per_seq - 1), 0)

    def prev_blk(b, t, s):
        return (b * blk_per_seq + jnp.maximum(t * blk_per_tile - 1, 0), 0)

    def next_blk(b, t, s):
        return (b * blk_per_seq + jnp.minimum((t + 1) * blk_per_tile, blk_per_seq - 1), 0)

    def batch(b, t, s):
        return (b, 0)

    def whole(b, t, s):
        return (0, 0)

    def flip(index_map):
        return lambda b, t, s: index_map(b, t, s)[::-1]

    in_specs = [
        pl.BlockSpec((tile, D_MODEL), main),
        _mod_spec(layer, lambda b, t, s: mod_base + mod_stride * b),
        pl.BlockSpec((tile, POOL_WIDTH), main),
        pl.BlockSpec((POOL_HALO, POOL_WIDTH), prev8),
        pl.BlockSpec((POOL_HALO, POOL_WIDTH), next8),
        pl.BlockSpec((tile, ATTN_WIDTH), main),
    ]
    args = [h, mods, u, u, u, q]
    if local:
        in_specs += [
            pl.BlockSpec((2 * KV_WIDTH, tile), flip(main)),
            pl.BlockSpec((2 * KV_WIDTH, BLOCK), flip(prev_blk)),
            pl.BlockSpec((2 * KV_WIDTH, BLOCK), flip(next_blk)),
            pl.BlockSpec((tile, 2 * KV_WIDTH), main),
            pl.BlockSpec((BLOCK, 2 * KV_WIDTH), prev_blk),
            pl.BlockSpec((BLOCK, 2 * KV_WIDTH), next_blk),
        ]
        args += [kt, kt, kt, vv, vv, vv]
    in_specs += [
        pl.BlockSpec((2 * KV_WIDTH, ctx_len), flip(batch)),
        pl.BlockSpec((ctx_len, 2 * KV_WIDTH), batch),
        pl.BlockSpec((POOL_GROUPS, POOL_GROUP_DIM, POOL_GROUP_DIM), lambda b, t, s: (0, 0, 0)),
        pl.BlockSpec((1, POOL_WIDTH), whole),
        pl.BlockSpec((MIX_WIDTH, D_MODEL), whole),
    ]
    args += [ktc, vvc, w_pool, pool_scale, w_out]
    return pl.pallas_call(
        functools.partial(_mix_kernel, tile=tile, seq_len=seq_len, local=local),
        grid_spec=pltpu.PrefetchScalarGridSpec(
            num_scalar_prefetch=1,
            grid=(n_batch, tiles),
            in_specs=in_specs,
            out_specs=pl.BlockSpec((tile, D_MODEL), main),
        ),
        out_shape=jax.ShapeDtypeStruct((n_tok, D_MODEL), F32),
        compiler_params=pltpu.CompilerParams(
            dimension_semantics=("arbitrary", "arbitrary"),
            vmem_limit_bytes=VMEM_LIMIT_BYTES),
        name="mix_latent" if local else "mix_context",
    )(sink, *args)


def _permute_w_in(w):
    d = w.shape[0]
    half = ROPE_AXIS_DIM
    wq = w[:, POOL_WIDTH:MIX_WIDTH].reshape(d, N_KV_HEADS, Q_GROUP, 2, half)
    wq = wq.transpose(0, 2, 3, 1, 4).reshape(d, ATTN_WIDTH)
    wk = w[:, MIX_WIDTH:MIX_WIDTH + KV_WIDTH].reshape(d, N_KV_HEADS, 2, half)
    wk = wk.transpose(0, 2, 1, 3).reshape(d, KV_WIDTH)
    return jnp.concatenate([w[:, :POOL_WIDTH], wq, wk, w[:, MIX_WIDTH + KV_WIDTH:]], axis=1)


def _permute_w_out(w):
    wa = w[POOL_WIDTH:].reshape(N_KV_HEADS, Q_GROUP, HEAD_DIM, w.shape[1])
    wa = wa.transpose(1, 0, 2, 3).reshape(ATTN_WIDTH, w.shape[1])
    return jnp.concatenate([w[:POOL_WIDTH], wa], axis=0)


def _rope_tables(seq_len):
    rows = seq_len // GRID_W
    row = np.repeat(np.arange(rows), GRID_W).astype(np.float64)
    col = np.tile(np.arange(GRID_W), rows).astype(np.float64)
    inv = ROPE_BASE ** (-np.arange(0, ROPE_AXIS_DIM, 2, dtype=np.float64) / ROPE_AXIS_DIM)
    ang = np.concatenate([row[:, None] * inv, col[:, None] * inv], axis=-1)
    cos, sin = np.cos(ang), np.sin(ang)
    cos_t = np.concatenate([cos, cos, cos, cos], axis=-1).astype(np.float32)
    sin_t = np.concatenate([-sin, -sin, sin, sin], axis=-1).astype(np.float32)
    return jnp.asarray(cos_t), jnp.asarray(sin_t)


def kernel(x, c, ctx, c_ctx, w_mod, b_mod, norm_ffn1, w_ffn1_in, w_ffn1_out, norm_mix, w_in,
           w_pool, pool_scale, sink, w_out, norm_ffn2, w_ffn2_in, w_ffn2_out, norm_final):
    n_batch, seq_len, _ = x.shape
    ctx_len = ctx.shape[1]
    depth = w_mod.shape[0]
    assert n_batch + 1 <= MOD_ROWS
    assert seq_len % MIX_TILE == 0 and seq_len % PROJ_TILE == 0 and seq_len % FFN_TILE == 0
    assert (n_batch * ctx_len) % FFN_TILE == 0 and ctx_len % BLOCK == 0

    cos_t, sin_t = _rope_tables(seq_len)
    cond = jnp.concatenate(
        [c, c_ctx[None, :], jnp.zeros((MOD_ROWS - n_batch - 1, D_MODEL), F32)], axis=0)
    mods = _mod_call(cond, w_mod, b_mod).reshape(depth, MOD_ROWS, N_MOD, D_MODEL)

    h = x.reshape(n_batch * seq_len, D_MODEL)
    hc = ctx.reshape(n_batch * ctx_len, D_MODEL)
    ffn_cfg = dict(n_lat_tok=n_batch * seq_len, tiles_per_batch=seq_len // FFN_TILE,
                   ctx_mod_row=n_batch, tile=FFN_TILE)
    g1, g2 = norm_ffn1[:, None, :], norm_ffn2[:, None, :]
    w_next = [w_ffn1_in[0].astype(BF16), w_ffn1_out[0].astype(BF16)]

    for l in range(depth):
        last = l == depth - 1
        w_in_l = _permute_w_in(w_in[l]).astype(BF16)
        w_out_l = _permute_w_out(w_out[l]).astype(BF16)
        w_pool_l = w_pool[l].astype(BF16)
        pscale = pool_scale[l][None, :]
        gm = norm_mix[l][None, :]

        h, hc, w_next = _ffn_call(h, hc, mods, g1[l], *w_next, layer=l, row0=0,
                                  cast=((w_ffn2_in, l), (w_ffn2_out, l)), **ffn_cfg)

        u, q, kt, vv = _proj_call(h, mods, gm, w_in_l, cos_t, sin_t, n_tok=n_batch * seq_len,
                                  layer=l, mod_base=0, tiles_per_mod=seq_len // PROJ_TILE,
                                  tile=PROJ_TILE, rope=True)
        uc, qc, ktc, vvc = _proj_call(hc, mods, gm, w_in_l, None, None, n_tok=n_batch * ctx_len,
                                      layer=l, mod_base=n_batch, tiles_per_mod=1,
                                      tile=n_batch * ctx_len, rope=False)
        h = _mix_call(sink[l], h, mods, u, q, kt, vv, ktc, vvc, w_pool_l, pscale, w_out_l,
                      layer=l, mod_base=0, mod_stride=1, n_batch=n_batch, seq_len=seq_len,
                      ctx_len=ctx_len, tile=MIX_TILE, local=True)
        if not last:
            hc = _mix_call(sink[l], hc, mods, uc, qc, None, None, ktc, vvc, w_pool_l, pscale,
                           w_out_l, layer=l, mod_base=n_batch, mod_stride=0, n_batch=n_batch,
                           seq_len=ctx_len, ctx_len=ctx_len, tile=ctx_len, local=False)
            h, hc, w_next = _ffn_call(h, hc, mods, g2[l], *w_next, layer=l, row0=6,
                                      cast=((w_ffn1_in, l + 1), (w_ffn1_out, l + 1)), **ffn_cfg)
        else:
            h, _, _ = _ffn_call(h, None, mods, g2[l], *w_next, layer=l, row0=6,
                                final_gain=norm_final[None, :], **ffn_cfg)
    return h.reshape(n_batch, seq_len, D_MODEL)
```
